```python
import math
import jax
import jax.numpy as jnp
from jax import lax
import numpy as np

D_MODEL = 1024
BATCH = 2
SEQ = 16384
DEPTH = 2

D_MIX = D_MODEL
HG_HEADS = 4
HG_DK = 64
HG_DV = 64
HG_QK = HG_HEADS * HG_DK
HG_W = HG_HEADS * HG_DV
GD_HEADS = 4
GD_DK = 128
GD_DV = 128
GD_QK = GD_HEADS * GD_DK
GD_W = GD_HEADS * GD_DV
RG_W = D_MIX - HG_W - GD_W
RG_BLOCKS = 4
RG_BD = RG_W // RG_BLOCKS
RG_C = 8.0
CONV_W = 4
D_FF = 2816
CHUNK = 64
EPS = 1e-6
SPLIT_SIZES = (HG_QK, HG_QK, HG_W, HG_W, GD_QK, GD_QK, GD_W, GD_W, GD_HEADS, GD_HEADS, RG_W, RG_W)
SPLIT_POINTS = tuple(int(p) for p in np.cumsum(SPLIT_SIZES)[:-1])
N_IN = int(sum(SPLIT_SIZES))

kernel_name = 'hybrid_hgrn2_gdn_rglru_macaron'


def rmsnorm(x, w):
    xf = x.astype(jnp.float32)
    y = xf * lax.rsqrt(jnp.mean(xf * xf, axis=-1, keepdims=True) + EPS)
    return (y * w.astype(jnp.float32)).astype(x.dtype)


def head_rmsnorm(o, w):
    return o * lax.rsqrt(jnp.mean(o * o, axis=-1, keepdims=True) + EPS) * w.astype(jnp.float32)


def l2norm(t):
    return t * lax.rsqrt(jnp.sum(t * t, axis=-1, keepdims=True) + EPS)


def causal_depthwise_conv(x, w):
    return lax.conv_general_dilated(
        x, w[:, None, :].astype(x.dtype), window_strides=(1,), padding=[(CONV_W - 1, 0)],
        dimension_numbers=('NWC', 'WIO', 'NWC'), feature_group_count=x.shape[-1])


def to_chunks(t):
    b, l, h, d = t.shape
    return t.reshape(b, l // CHUNK, CHUNK, h, d).transpose(1, 0, 3, 2, 4)


def from_chunks(t):
    n, b, h, c, d = t.shape
    return t.transpose(1, 0, 3, 2, 4).reshape(b, n * c, h, d)


def swiglu(h, w_gate, w_up, w_down):
    return (jax.nn.silu(h @ w_gate) * (h @ w_up)) @ w_down


def hgrn2_chunked(q, k, v, logf):
    qc, kc, vc, gc = to_chunks(q), to_chunks(k), to_chunks(v), to_chunks(logf)
    _, b, h, _, dk = qc.shape
    dv = vc.shape[-1]
    causal = jnp.tril(jnp.ones((CHUNK, CHUNK), dtype=bool))[:, :, None]

    def step(state, inp):
        qi, ki, vi, gi = inp
        cum = jnp.cumsum(gi, axis=2)
        diff = cum[:, :, :, None, :] - cum[:, :, None, :, :]
        decay = jnp.exp(jnp.where(causal, diff, -jnp.inf))
        scores = jnp.einsum('bhtd,bhsd,bhtsd->bhts', qi, ki, decay)
        o = (jnp.einsum('bhts,bhsv->bhtv', scores, vi)
             + jnp.einsum('bhtd,bhdv->bhtv', qi * jnp.exp(cum), state))
        last = cum[:, :, -1:, :]
        state = (state * jnp.exp(last[:, :, 0, :, None])
                 + jnp.einsum('bhsd,bhsv->bhdv', ki * jnp.exp(last - cum), vi))
        return state, o

    s0 = jnp.zeros((b, h, dk, dv), jnp.float32)
    _, o = lax.scan(step, s0, (qc, kc, vc, gc))
    return from_chunks(o)


def gated_delta_chunked(q, k, v, g, beta):
    qc, kc, vc = to_chunks(q), to_chunks(k), to_chunks(v)
    gc = to_chunks(g[..., None])[..., 0]
    bc = to_chunks(beta[..., None])[..., 0]
    _, b, h, _, dk = qc.shape
    dv = vc.shape[-1]
    incl = jnp.tril(jnp.ones((CHUNK, CHUNK), dtype=bool))
    strict = jnp.tril(jnp.ones((CHUNK, CHUNK), dtype=bool), k=-1)
    eye = jnp.eye(CHUNK, dtype=jnp.float32)

    def step(state, inp):
        qi, ki, vi, gi, bi = inp
        cum = jnp.cumsum(gi, axis=-1)
        decay = jnp.exp(jnp.where(incl, cum[..., :, None] - cum[..., None, :], -jnp.inf))
        kb = ki * bi[..., None]
        lower = jnp.where(strict, jnp.einsum('bhtd,bhsd->bhts', kb, ki) * decay, 0.0)
        rhs = jnp.concatenate([vi * bi[..., None], kb * jnp.exp(cum)[..., None]], axis=-1)
        sol = lax.linalg.triangular_solve(eye + lower, rhs, left_side=True, lower=True,
                                          unit_diagonal=True)
        u, w = sol[..., :dv], sol[..., dv:]
        v_new = u - jnp.einsum('bhtd,bhdv->bhtv', w, state)
        scores = jnp.einsum('bhtd,bhsd->bhts', qi, ki) * decay
        o = (jnp.einsum('bhtd,bhdv->bhtv', qi * jnp.exp(cum)[..., None], state)
             + jnp.einsum('bhts,bhsv->bhtv', scores, v_new))
        last = cum[..., -1:]
        state = (state * jnp.exp(last)[..., None]
                 + jnp.einsum('bhsd,bhsv->bhdv', ki * jnp.exp(last - cum)[..., None], v_new))
        return state, o

    s0 = jnp.zeros((b, h, dk, dv), jnp.float32)
    _, o = lax.scan(step, s0, (qc, kc, vc, gc, bc))
    return from_chunks(o)


def rg_lru(xc, wr, br, wi, bi, lam):
    b, l, _ = xc.shape
    xb = xc.reshape(b, l, RG_BLOCKS, RG_BD)
    r = jax.nn.sigmoid(jnp.einsum('blni,nio->blno', xb, wr.astype(jnp.float32)).reshape(b, l, RG_W)
                       + br.astype(jnp.float32))
    ig = jax.nn.sigmoid(jnp.einsum('blni,nio->blno', xb, wi.astype(jnp.float32)).reshape(b, l, RG_W)
                        + bi.astype(jnp.float32))
    log_a = -RG_C * r * jax.nn.softplus(-lam.astype(jnp.float32))
    a = jnp.exp(log_a)
    bx = jnp.sqrt(-jnp.expm1(2.0 * log_a)) * (ig * xc)

    def combine(c1, c2):
        a1, b1 = c1
        a2, b2 = c2
        return a1 * a2, a2 * b1 + b2

    _, hs = lax.associative_scan(combine, (a, bx), axis=1)
    return hs


def hybrid_mixer(h, lb, w_in, hg_norm_w, gd_conv_w, gd_a_log, gd_dt_bias, gd_norm_w,
                 rg_conv_w, rg_conv_b, rg_wr, rg_br, rg_wi, rg_bi, rg_lambda, w_out):
    b, l, _ = h.shape
    f32 = jnp.float32
    proj = (h @ w_in).astype(f32)
    (hg_q, hg_f, hg_i, hg_g, gd_q, gd_k, gd_v, gd_z, gd_b, gd_a,
     rg_x, rg_gate) = jnp.split(proj, SPLIT_POINTS, axis=-1)

    lbf = lb.reshape(HG_HEADS, HG_DK)
    q = jax.nn.silu(hg_q).reshape(b, l, HG_HEADS, HG_DK) * (HG_DK ** -0.5)
    f = lbf + (1.0 - lbf) * jax.nn.sigmoid(hg_f.reshape(b, l, HG_HEADS, HG_DK))
    o_hg = hgrn2_chunked(q, 1.0 - f, hg_i.reshape(b, l, HG_HEADS, HG_DV), jnp.log(f))
    o_hg = head_rmsnorm(o_hg, hg_norm_w) * jax.nn.silu(hg_g.reshape(b, l, HG_HEADS, HG_DV))

    qkv = jax.nn.silu(causal_depthwise_conv(jnp.concatenate([gd_q, gd_k, gd_v], axis=-1), gd_conv_w))
    cq, ck, cv = jnp.split(qkv, (GD_QK, 2 * GD_QK), axis=-1)
    cq = l2norm(cq.reshape(b, l, GD_HEADS, GD_DK)) * (GD_DK ** -0.5)
    ck = l2norm(ck.reshape(b, l, GD_HEADS, GD_DK))
    beta = jax.nn.sigmoid(gd_b)
    g = -jnp.exp(gd_a_log.astype(f32)) * jax.nn.softplus(gd_a + gd_dt_bias.astype(f32))
    o_gd = gated_delta_chunked(cq, ck, cv.reshape(b, l, GD_HEADS, GD_DV), g, beta)
    o_gd = head_rmsnorm(o_gd, gd_norm_w) * jax.nn.silu(gd_z.reshape(b, l, GD_HEADS, GD_DV))

    xc = causal_depthwise_conv(rg_x, rg_conv_w) + rg_conv_b.astype(f32)
    o_rg = rg_lru(xc, rg_wr, rg_br, rg_wi, rg_bi, rg_lambda) * jax.nn.gelu(rg_gate, approximate=True)

    y = jnp.concatenate([o_hg.reshape(b, l, HG_W), o_gd.reshape(b, l, GD_W), o_rg], axis=-1)
    return y.astype(h.dtype) @ w_out


def setup_inputs(seed: int = 0) -> dict:
    key = jax.random.key(seed)
    ks = jax.random.split(key, 32)
    f32 = jnp.float32

    def nrm(k, shape, scale):
        return jax.random.normal(k, shape, f32) * scale

    def gain(k, shape):
        return 1.0 + 0.02 * jax.random.normal(k, shape, f32)

    dt = jnp.exp(jax.random.uniform(ks[12], (DEPTH, GD_HEADS), f32, math.log(1e-3), math.log(1e-1)))
    a0 = jax.random.uniform(ks[19], (DEPTH, RG_W), f32, 0.9, 0.999) ** (1.0 / RG_C)
    return {
        'x': nrm(ks[0], (BATCH, SEQ, D_MODEL), 1.0),
        'norm_ffn1': gain(ks[1], (DEPTH, D_MODEL)),
        'ffn1_gate': nrm(ks[2], (DEPTH, D_MODEL, D_FF), D_MODEL ** -0.5),
        'ffn1_up': nrm(ks[3], (DEPTH, D_MODEL, D_FF), D_MODEL ** -0.5),
        'ffn1_down': nrm(ks[4], (DEPTH, D_FF, D_MODEL), D_FF ** -0.5),
        'norm_mix': gain(ks[5], (DEPTH, D_MODEL)),
        'w_in': nrm(ks[6], (DEPTH, D_MODEL, N_IN), D_MODEL ** -0.5),
        'hg_lb': nrm(ks[7], (DEPTH, HG_QK), 1.0),
        'hg_norm_w': gain(ks[8], (DEPTH, HG_DV)),
        'gd_conv_w': nrm(ks[9], (DEPTH, CONV_W, 2 * GD_QK + GD_W), CONV_W ** -0.5),
        'gd_a_log': jnp.log(jax.random.uniform(ks[10], (DEPTH, GD_HEADS), f32, 1.0, 16.0)),
        'gd_dt_bias': dt + jnp.log(-jnp.expm1(-dt)),
        'gd_norm_w': gain(ks[11], (DEPTH, GD_DV)),
        'rg_conv_w': nrm(ks[13], (DEPTH, CONV_W, RG_W), CONV_W ** -0.5),
        'rg_conv_b': nrm(ks[14], (DEPTH, RG_W), 0.02),
        'rg_wr': nrm(ks[15], (DEPTH, RG_BLOCKS, RG_BD, RG_BD), RG_BD ** -0.5),
        'rg_br': nrm(ks[16], (DEPTH, RG_W), 0.1),
        'rg_wi': nrm(ks[17], (DEPTH, RG_BLOCKS, RG_BD, RG_BD), RG_BD ** -0.5),
        'rg_bi': nrm(ks[18], (DEPTH, RG_W), 0.1),
        'rg_lambda': jnp.log(a0) - jnp.log1p(-a0),
        'w_out': nrm(ks[20], (DEPTH, D_MIX, D_MODEL), D_MIX ** -0.5),
        'norm_ffn2': gain(ks[21], (DEPTH, D_MODEL)),
        'ffn2_gate': nrm(ks[22], (DEPTH, D_MODEL, D_FF), D_MODEL ** -0.5),
        'ffn2_up': nrm(ks[23], (DEPTH, D_MODEL, D_FF), D_MODEL ** -0.5),
        'ffn2_down': nrm(ks[24], (DEPTH, D_FF, D_MODEL), D_FF ** -0.5),
        'norm_final': gain(ks[25], (D_MODEL,)),
    }


def reference(x, norm_ffn1, ffn1_gate, ffn1_up, ffn1_down, norm_mix, w_in, hg_lb, hg_norm_w,
              gd_conv_w, gd_a_log, gd_dt_bias, gd_norm_w, rg_conv_w, rg_conv_b, rg_wr, rg_br,
              rg_wi, rg_bi, rg_lambda, w_out, norm_ffn2, ffn2_gate, ffn2_up, ffn2_down, norm_final):
    lb_all = jnp.cumsum(jax.nn.softmax(hg_lb.astype(jnp.float32), axis=0), axis=0)
    lb_all = lb_all - lb_all[0]
    for layer in range(DEPTH):
        x = x + 0.5 * swiglu(rmsnorm(x, norm_ffn1[layer]), ffn1_gate[layer], ffn1_up[layer],
                             ffn1_down[layer])
        x = x + hybrid_mixer(rmsnorm(x, norm_mix[layer]), lb_all[layer], w_in[layer], hg_norm_w[layer],
                             gd_conv_w[layer], gd_a_log[layer], gd_dt_bias[layer], gd_norm_w[layer],
                             rg_conv_w[layer], rg_conv_b[layer], rg_wr[layer], rg_br[layer],
                             rg_wi[layer], rg_bi[layer], rg_lambda[layer], w_out[layer])
        x = x + 0.5 * swiglu(rmsnorm(x, norm_ffn2[layer]), ffn2_gate[layer], ffn2_up[layer],
                             ffn2_down[layer])
    return rmsnorm(x, norm_final)
```

```python
import functools

import numpy as np
import jax
import jax.numpy as jnp
from jax import lax
from jax.experimental import pallas as pl
from jax.experimental.pallas import tpu as pltpu

F32 = jnp.float32
BF16 = jnp.bfloat16

EPS = 1e-6
CHUNK = 64
CONV_W = 4
HG_HEADS, HG_DK, HG_DV = 4, 64, 64
GD_HEADS, GD_DK, GD_DV = 4, 128, 128
RG_BLOCKS = 4
RG_C = 8.0
HG_W = HG_HEADS * HG_DV
HG_QK = HG_HEADS * HG_DK
GD_W = GD_HEADS * GD_DV
GD_QK = GD_HEADS * GD_DK
LANES = 128
HALO = 8
VMEM_LIMIT = 56 * 1024 * 1024

HG_LEVELS = (32, 16, 8, 4, 2, 1)


def _dot(a, b):
    return jnp.dot(a, b, preferred_element_type=F32)


def _dot_nt(a, b):
    return lax.dot_general(a, b, (((1,), (1,)), ((), ())), preferred_element_type=F32)


def _dot_tn(a, b):
    return lax.dot_general(a, b, (((0,), (0,)), ((), ())), preferred_element_type=F32)


def _split3(x):
    hi = x.astype(BF16)
    r = x - hi.astype(F32)
    mid = r.astype(BF16)
    lo = (r - mid.astype(F32)).astype(BF16)
    return hi, mid, lo


def _dot_exact_left(m, x):
    hi, mid, lo = _split3(x)
    return _dot(m, hi) + _dot(m, mid) + _dot(m, lo)


def _dot_exact_right(x, m):
    hi = x.astype(BF16)
    lo = (x - hi.astype(F32)).astype(BF16)
    return _dot(hi, m) + _dot(lo, m)


def _rms(x, w):
    ms = jnp.mean(x * x, axis=-1, keepdims=True)
    return x * lax.rsqrt(ms + EPS) * w


def _sigmoid(x):
    return 1.0 / (1.0 + jnp.exp(-x))


def _silu(x):
    return x * _sigmoid(x)


def _softplus(x):
    return jnp.maximum(x, 0.0) + jnp.log1p(jnp.exp(-jnp.abs(x)))


def _ffn_kernel(*refs, has_mix, has_final, fc):
    it = iter(refs)
    x_ref = next(it)
    if has_mix:
        ohg_ref, ogd_ref, org_ref, wout_ref = next(it), next(it), next(it), next(it)
    nw_ref, wg_ref, wu_ref, wd_ref = next(it), next(it), next(it), next(it)
    fnw_ref = next(it) if has_final else None
    o_ref = next(it)
    a_ref = next(it)

    x = x_ref[...]
    if has_mix:
        x = (x + _dot(ohg_ref[...].astype(BF16), wout_ref[0:HG_W, :])
             + _dot(ogd_ref[...].astype(BF16), wout_ref[HG_W:HG_W + GD_W, :])
             + _dot(org_ref[...].astype(BF16), wout_ref[HG_W + GD_W:, :]))
    h = _rms(x, nw_ref[...]).astype(BF16)
    d_ff = wg_ref.shape[1]
    for c in range(d_ff // fc):
        sl = slice(c * fc, (c + 1) * fc)
        g = _dot(h, wg_ref[:, sl])
        u = _dot(h, wu_ref[:, sl])
        a_ref[:, sl] = (_silu(g) * u).astype(BF16)
    y = x + 0.5 * _dot(a_ref[...], wd_ref[...])
    if has_final:
        y = _rms(y, fnw_ref[...])
    o_ref[...] = y


def _const_spec(shape):
    nd = len(shape)
    return pl.BlockSpec(shape, lambda *_: (0,) * nd, pipeline_mode=pl.Buffered(1))


def _ffn_call(x, mix, nw, wg, wu, wd, fnw, *, tm):
    t, d = x.shape
    d_ff = wg.shape[1]
    has_mix, has_final = mix is not None, fnw is not None
    row = lambda w: pl.BlockSpec((tm, w), lambda i: (i, 0))
    args, specs = [x], [row(d)]
    if has_mix:
        ohg, ogd, org, wout = mix
        args += [ohg, ogd, org, wout]
        specs += [row(HG_W), row(GD_W), row(org.shape[1]), _const_spec(wout.shape)]
    args += [nw, wg, wu, wd]
    specs += [_const_spec(nw.shape), _const_spec(wg.shape), _const_spec(wu.shape), _const_spec(wd.shape)]
    if has_final:
        args.append(fnw)
        specs.append(_const_spec(fnw.shape))
    kern = functools.partial(_ffn_kernel, has_mix=has_mix, has_final=has_final, fc=256)
    return pl.pallas_call(
        kern,
        grid=(t // tm,),
        in_specs=specs,
        out_specs=row(d),
        out_shape=jax.ShapeDtypeStruct((t, d), F32),
        scratch_shapes=[pltpu.VMEM((tm, d_ff), BF16)],
        compiler_params=pltpu.CompilerParams(dimension_semantics=("arbitrary",),
                                             vmem_limit_bytes=VMEM_LIMIT),
        name="ffn_mix" if has_mix else "ffn",
    )(*args)


def _proj_kernel(x_ref, nw_ref, whg_ref, wgd_ref, wrg_ref, wba_ref, lbraw_ref, gdcw_ref,
                 alog_ref, dtb_ref, rgcw_ref, rgcb_ref, wgate_ref, bgate_ref, lam_ref,
                 hg_ref, gd_ref, ba_ref, rg_ref, gdpad_ref, rgpad_ref,
                 *, layer, tiles_per_batch, tm):
    i = pl.program_id(0)
    first = (i % tiles_per_batch) == 0
    h = _rms(x_ref[...], nw_ref[...]).astype(BF16)

    depth = lbraw_ref.shape[0]
    rows = [lbraw_ref[r:r + 1, :] for r in range(depth)]
    mx = functools.reduce(jnp.maximum, rows)
    es = [jnp.exp(r - mx) for r in rows]
    tot = functools.reduce(lambda a, b: a + b, es)
    cums, run = [], None
    for e in es:
        run = e / tot if run is None else run + e / tot
        cums.append(run)
    lb = cums[layer] - cums[0]
    p = _dot(h, whg_ref[...])
    hg_ref[:, 0:HG_QK] = _silu(p[:, 0:HG_QK]) * (HG_DK ** -0.5)
    f = lb + (1.0 - lb) * _sigmoid(p[:, HG_QK:2 * HG_QK])
    hg_ref[:, HG_QK:2 * HG_QK] = jnp.log(f)
    hg_ref[:, 2 * HG_QK:2 * HG_QK + HG_W] = p[:, 2 * HG_QK:2 * HG_QK + HG_W]
    hg_ref[:, 2 * HG_QK + HG_W:] = _silu(p[:, 2 * HG_QK + HG_W:])

    g = _dot(h, wgd_ref[...])
    ncv = 2 * GD_QK + GD_W

    @pl.when(first)
    def _():
        gdpad_ref[0:HALO, :] = jnp.zeros((HALO, ncv), F32)
        rgpad_ref[0:HALO, :] = jnp.zeros((HALO, rgpad_ref.shape[1]), F32)

    @pl.when(jnp.logical_not(first))
    def _():
        gdpad_ref[0:HALO, :] = gdpad_ref[tm:tm + HALO, :]
        rgpad_ref[0:HALO, :] = rgpad_ref[tm:tm + HALO, :]

    gdpad_ref[HALO:HALO + tm, :] = g[:, 0:ncv]
    y = None
    for j in range(CONV_W):
        off = HALO - (CONV_W - 1) + j
        term = gdcw_ref[j:j + 1, :] * gdpad_ref[off:off + tm, :]
        y = term if y is None else y + term
    c = _silu(y)
    for hh in range(GD_HEADS):
        sq = c[:, hh * GD_DK:(hh + 1) * GD_DK]
        sk = c[:, GD_QK + hh * GD_DK:GD_QK + (hh + 1) * GD_DK]
        gd_ref[:, hh * GD_DK:(hh + 1) * GD_DK] = (
            sq * lax.rsqrt(jnp.sum(sq * sq, axis=-1, keepdims=True) + EPS) * (GD_DK ** -0.5))
        gd_ref[:, GD_QK + hh * GD_DK:GD_QK + (hh + 1) * GD_DK] = (
            sk * lax.rsqrt(jnp.sum(sk * sk, axis=-1, keepdims=True) + EPS))
    gd_ref[:, 2 * GD_QK:ncv] = c[:, 2 * GD_QK:ncv]
    gd_ref[:, ncv:] = _silu(g[:, ncv:])

    ba = _dot(h, wba_ref[...])
    lane = lax.broadcasted_iota(jnp.int32, (1, LANES), 1)
    gg = -jnp.exp(alog_ref[...]) * _softplus(ba + dtb_ref[...])
    ba_ref[...] = jnp.where(lane < GD_HEADS, _sigmoid(ba), gg)

    r = _dot(h, wrg_ref[...])
    rgw = rgpad_ref.shape[1]
    rgpad_ref[HALO:HALO + tm, :] = r[:, 0:rgw]
    xc = None
    for j in range(CONV_W):
        off = HALO - (CONV_W - 1) + j
        term = rgcw_ref[j:j + 1, :] * rgpad_ref[off:off + tm, :]
        xc = term if xc is None else xc + term
    xc = xc + rgcb_ref[...]
    gates = _dot(xc.astype(BF16), wgate_ref[...]) + bgate_ref[...]
    rr = _sigmoid(gates[:, 0:rgw])
    ig = _sigmoid(gates[:, rgw:])
    log_a = -RG_C * rr * _softplus(-lam_ref[...])
    a = jnp.exp(log_a)
    rg_ref[:, 0:rgw] = a
    rg_ref[:, rgw:2 * rgw] = jnp.sqrt(-jnp.tanh(log_a) * (a * a + 1.0)) * (ig * xc)
    rg_ref[:, 2 * rgw:] = jax.nn.gelu(r[:, rgw:], approximate=True)


def _proj_call(x, consts, *, layer, tiles_per_batch, tm):
    t, d = x.shape
    rgw = consts[9].shape[1]
    row = lambda w: pl.BlockSpec((tm, w), lambda i: (i, 0))
    kern = functools.partial(_proj_kernel, layer=layer, tiles_per_batch=tiles_per_batch, tm=tm)
    ncv = 2 * GD_QK + GD_W
    return pl.pallas_call(
        kern,
        grid=(t // tm,),
        in_specs=[row(d)] + [_const_spec(c.shape) for c in consts],
        out_specs=[row(2 * HG_QK + 2 * HG_W), row(ncv + GD_W), row(LANES), row(3 * rgw)],
        out_shape=[jax.ShapeDtypeStruct((t, 2 * HG_QK + 2 * HG_W), F32),
                   jax.ShapeDtypeStruct((t, ncv + GD_W), F32),
                   jax.ShapeDtypeStruct((t, LANES), F32),
                   jax.ShapeDtypeStruct((t, 3 * rgw), F32)],
        scratch_shapes=[pltpu.VMEM((tm + HALO, ncv), F32), pltpu.VMEM((tm + HALO, rgw), F32)],
        compiler_params=pltpu.CompilerParams(dimension_semantics=("arbitrary",),
                                             vmem_limit_bytes=VMEM_LIMIT),
        name="mixer_proj",
    )(x, *consts)


def _hg_constants():
    c = CHUNK
    mats, masks = [], []
    for hsz in HG_LEVELS:
        m = np.zeros((c, c), np.float32)
        pm = np.zeros((c, c), np.float32)
        for t in range(c):
            blk = (t // (2 * hsz)) * (2 * hsz)
            b = blk + hsz - 1
            if t - blk >= hsz:
                m[t, b + 1:t + 1] = 1.0
                pm[t, blk:blk + hsz] = 1.0
            else:
                m[t, t + 1:b + 1] = 1.0
        mats.append(m)
        masks.append(pm)
    mats.append(np.tril(np.ones((c, c), np.float32)))
    masks.append(np.eye(c, dtype=np.float32))
    mall = np.concatenate(mats, axis=0)
    pmask = np.stack(masks, axis=0)
    hid = np.arange(HG_W) // HG_DV
    bd = (hid[:, None] == hid[None, :]).astype(np.float32)
    return mall, pmask, bd


def _hg_kernel(pre_ref, mall_ref, pmask_ref, bd_ref, avg_ref, nw_ref, o_ref, st_ref, *, tl):
    @pl.when(pl.program_id(1) == 0)
    def _():
        st_ref[...] = jnp.zeros(st_ref.shape, F32)

    nlev = len(HG_LEVELS)
    rowid = lax.broadcasted_iota(jnp.int32, (CHUNK, 1), 0)
    lane_head = lax.broadcasted_iota(jnp.int32, (1, HG_W), 1) // HG_DV
    head_masks = [lane_head == hh for hh in range(HG_HEADS)]
    mall = mall_ref[...]
    bd = bd_ref[...]
    avg = avg_ref[...]
    nw = nw_ref[...]

    def body(ci, carry):
        r0 = pl.multiple_of(ci * CHUNK, CHUNK)
        pre = pre_ref[pl.ds(r0, CHUNK), :]
        q = pre[:, 0:HG_QK]
        lf = pre[:, HG_QK:2 * HG_QK]
        v = pre[:, 2 * HG_QK:2 * HG_QK + HG_W]
        gate = pre[:, 2 * HG_QK + HG_W:]
        k = 1.0 - jnp.exp(lf)
        dl = _dot_exact_left(mall, lf)
        el = jnp.exp(dl)
        cum = dl[nlev * CHUNK:, :]
        ecum = el[nlev * CHUNK:, :]

        sc = []
        for hh in range(HG_HEADS):
            xq = jnp.where(head_masks[hh], q, 0.0).astype(BF16)
            xk = jnp.where(head_masks[hh], k, 0.0).astype(BF16)
            sc.append(_dot_nt(xq, xk) * pmask_ref[nlev])
        for li, hsz in enumerate(HG_LEVELS):
            second = (rowid & hsz) != 0
            xl = jnp.where(second, q, k) * el[li * CHUNK:(li + 1) * CHUNK, :]
            for hh in range(HG_HEADS):
                xh = jnp.where(head_masks[hh], xl, 0.0).astype(BF16)
                sc[hh] = sc[hh] + _dot_nt(xh, xh) * pmask_ref[li]
        vb = v.astype(BF16)
        o = _dot_nt((q * ecum).astype(BF16), st_ref[...].astype(BF16))
        for hh in range(HG_HEADS):
            o = o + jnp.where(head_masks[hh], _dot(sc[hh].astype(BF16), vb), 0.0)

        last = cum[CHUNK - 1:CHUNK, :]
        kt = (k * jnp.exp(last - cum)).astype(BF16)
        st_ref[...] = st_ref[...] * ecum[CHUNK - 1:CHUNK, :] + _dot_tn(vb, kt) * bd

        ms = _dot_exact_right(o * o, avg)
        o_ref[pl.ds(r0, CHUNK), :] = o * lax.rsqrt(ms + EPS) * nw * gate
        return carry

    lax.fori_loop(0, tl // CHUNK, body, 0)


def _hg_call(pre, consts, *, batch, tl):
    t = pre.shape[0]
    nt = t // batch // tl
    kern = functools.partial(_hg_kernel, tl=tl)
    return pl.pallas_call(
        kern,
        grid=(batch, nt),
        in_specs=[pl.BlockSpec((tl, pre.shape[1]), lambda b, i: (b * nt + i, 0))]
        + [_const_spec(c.shape) for c in consts],
        out_specs=pl.BlockSpec((tl, HG_W), lambda b, i: (b * nt + i, 0)),
        out_shape=jax.ShapeDtypeStruct((t, HG_W), F32),
        scratch_shapes=[pltpu.VMEM((HG_W, HG_QK), F32)],
        compiler_params=pltpu.CompilerParams(dimension_semantics=("arbitrary", "arbitrary"),
                                             vmem_limit_bytes=VMEM_LIMIT),
        name="hgrn2",
    )(pre, *consts)


def _gd_kernel(pre_ref, ba_ref, ltri_ref, nw_ref, o_ref, s_ref, *, tl):
    @pl.when(pl.program_id(1) == 0)
    def _():
        s_ref[...] = jnp.zeros(s_ref.shape, F32)

    ti = lax.broadcasted_iota(jnp.int32, (CHUNK, CHUNK), 0)
    si = lax.broadcasted_iota(jnp.int32, (CHUNK, CHUNK), 1)
    incl = ti >= si
    strict = ti > si
    ltri = ltri_ref[...]
    nw = nw_ref[...]

    def body(ci, carry):
        r0 = pl.multiple_of(ci * CHUNK, CHUNK)
        pre = pre_ref[pl.ds(r0, CHUNK), :]
        ba = ba_ref[pl.ds(r0, CHUNK), :]
        cumc = _dot_exact_left(ltri, ba)
        cumr = cumc.T
        for hh in range(GD_HEADS):
            qh = pre[:, hh * GD_DK:(hh + 1) * GD_DK]
            kh = pre[:, GD_QK + hh * GD_DK:GD_QK + (hh + 1) * GD_DK]
            vh = pre[:, 2 * GD_QK + hh * GD_DV:2 * GD_QK + (hh + 1) * GD_DV]
            zg = pre[:, 2 * GD_QK + GD_W + hh * GD_DV:2 * GD_QK + GD_W + (hh + 1) * GD_DV]
            beta = ba[:, hh:hh + 1]
            cc = cumc[:, GD_HEADS + hh:GD_HEADS + hh + 1]
            cr = cumr[GD_HEADS + hh:GD_HEADS + hh + 1, :]
            decay = jnp.exp(jnp.where(incl, cc - cr, -1e30))
            kb = kh * beta
            khb = kh.astype(BF16)
            a = jnp.where(strict, _dot_nt(kb.astype(BF16), khb) * decay, 0.0)
            n = -a
            ap = a
            for _ in range(5):
                apb = ap.astype(BF16)
                ap = _dot(apb, apb)
                n = n + ap + _dot(n.astype(BF16), ap.astype(BF16))
            ecc = jnp.exp(cc)
            rhs = jnp.concatenate([vh * beta, kb * ecc], axis=1)
            sol = rhs + _dot(n.astype(BF16), rhs.astype(BF16))
            u = sol[:, 0:GD_DV]
            w = sol[:, GD_DV:]
            sh = s_ref[hh]
            shb = sh.astype(BF16)
            v_new = u - _dot(w.astype(BF16), shb)
            vnb = v_new.astype(BF16)
            qk = _dot_nt(qh.astype(BF16), khb) * decay
            o = _dot((qh * ecc).astype(BF16), shb) + _dot(qk.astype(BF16), vnb)
            last = cc[CHUNK - 1:CHUNK, :]
            kt = (kh * jnp.exp(last - cc)).astype(BF16)
            s_ref[hh] = sh * jnp.exp(last) + _dot_tn(kt, vnb)
            on = o * lax.rsqrt(jnp.mean(o * o, axis=-1, keepdims=True) + EPS) * nw * zg
            o_ref[pl.ds(r0, CHUNK), hh * GD_DV:(hh + 1) * GD_DV] = on
        return carry

    lax.fori_loop(0, tl // CHUNK, body, 0)


def _gd_call(pre, ba, consts, *, batch, tl):
    t = pre.shape[0]
    nt = t // batch // tl
    kern = functools.partial(_gd_kernel, tl=tl)
    rows = lambda w: pl.BlockSpec((tl, w), lambda b, i: (b * nt + i, 0))
    return pl.pallas_call(
        kern,
        grid=(batch, nt),
        in_specs=[rows(pre.shape[1]), rows(ba.shape[1])] + [_const_spec(c.shape) for c in consts],
        out_specs=rows(GD_W),
        out_shape=jax.ShapeDtypeStruct((t, GD_W), F32),
        scratch_shapes=[pltpu.VMEM((GD_HEADS, GD_DK, GD_DV), F32)],
        compiler_params=pltpu.CompilerParams(dimension_semantics=("arbitrary", "arbitrary"),
                                             vmem_limit_bytes=VMEM_LIMIT),
        name="gated_delta",
    )(pre, ba, *consts)


def _rg_kernel(pre_ref, o_ref, carry_ref, *, tl):
    @pl.when(pl.program_id(1) == 0)
    def _():
        carry_ref[...] = jnp.zeros(carry_ref.shape, F32)

    w = o_ref.shape[1]
    rowid = lax.broadcasted_iota(jnp.int32, (CHUNK, 1), 0)

    def body(ci, carry):
        r0 = pl.multiple_of(ci * CHUNK, CHUNK)
        pre = pre_ref[pl.ds(r0, CHUNK), :]
        a = pre[:, 0:w]
        b = pre[:, w:2 * w]
        gate = pre[:, 2 * w:]
        s = 1
        while s < CHUNK:
            keep = rowid >= s
            a_sh = jnp.where(keep, pltpu.roll(a, s, 0), 1.0)
            b_sh = jnp.where(keep, pltpu.roll(b, s, 0), 0.0)
            b = a * b_sh + b
            a = a * a_sh
            s *= 2
        hst = b + a * carry_ref[0:1, :]
        carry_ref[0:1, :] = hst[CHUNK - 1:CHUNK, :]
        o_ref[pl.ds(r0, CHUNK), :] = hst * gate
        return carry

    lax.fori_loop(0, tl // CHUNK, body, 0)


def _rg_call(pre, *, batch, tl):
    t = pre.shape[0]
    w = pre.shape[1] // 3
    nt = t // batch // tl
    kern = functools.partial(_rg_kernel, tl=tl)
    return pl.pallas_call(
        kern,
        grid=(batch, nt),
        in_specs=[pl.BlockSpec((tl, 3 * w), lambda b, i: (b * nt + i, 0))],
        out_specs=pl.BlockSpec((tl, w), lambda b, i: (b * nt + i, 0)),
        out_shape=jax.ShapeDtypeStruct((t, w), F32),
        scratch_shapes=[pltpu.VMEM((HALO, w), F32)],
        compiler_params=pltpu.CompilerParams(dimension_semantics=("arbitrary", "arbitrary"),
                                             vmem_limit_bytes=VMEM_LIMIT),
        name="rg_lru",
    )(pre)


def _block_diag(w):
    n, i, o = w.shape
    eye = jnp.eye(n, dtype=w.dtype)
    return (eye[:, None, :, None] * w[:, :, None, :]).reshape(n * i, n * o)


def _pick_tile(n, target):
    t = min(n, target)
    while n % t:
        t //= 2
    return t


def kernel(x, norm_ffn1, ffn1_gate, ffn1_up, ffn1_down, norm_mix, w_in, hg_lb, hg_norm_w, gd_conv_w, gd_a_log, gd_dt_bias, gd_norm_w, rg_conv_w, rg_conv_b, rg_wr, rg_br, rg_wi, rg_bi, rg_lambda, w_out, norm_ffn2, ffn2_gate, ffn2_up, ffn2_down, norm_final):
    batch, seq, d = x.shape
    depth = w_in.shape[0]
    t = batch * seq
    rgw = rg_conv_w.shape[-1]
    assert seq % CHUNK == 0 and d == HG_W + GD_W + rgw
    assert w_in.shape[-1] == 2 * HG_QK + 2 * HG_W + 2 * GD_QK + 2 * GD_W + 2 * GD_HEADS + 2 * rgw

    tm_ffn = _pick_tile(seq, 512)
    tm_proj = _pick_tile(seq, 256)
    tl = _pick_tile(seq, 512)

    mall, pmask, bd = _hg_constants()
    hg_consts_fixed = (jnp.asarray(mall, BF16), jnp.asarray(pmask, F32), jnp.asarray(bd, F32),
                       jnp.asarray(bd / HG_DV, BF16))
    ltri = jnp.asarray(np.tril(np.ones((CHUNK, CHUNK), np.float32)), BF16)

    o_hg = 0
    o_gd = o_hg + 2 * HG_QK + 2 * HG_W
    o_ba = o_gd + 2 * GD_QK + 2 * GD_W
    o_rg = o_ba + 2 * GD_HEADS

    xf = x.reshape(t, d)
    mix = None
    for layer in range(depth):
        row = lambda v: v.astype(F32).reshape(1, -1)
        xf = _ffn_call(xf, None, row(norm_ffn1[layer]), ffn1_gate[layer].astype(BF16),
                       ffn1_up[layer].astype(BF16), ffn1_down[layer].astype(BF16), None, tm=tm_ffn)

        wl = w_in[layer]
        pad_lanes = lambda v, off: jnp.zeros((1, LANES), F32).at[0, off:off + v.shape[0]].set(v.astype(F32))
        wba = jnp.zeros((d, LANES), F32).at[:, 0:2 * GD_HEADS].set(wl[:, o_ba:o_rg]).astype(BF16)
        wgate = jnp.concatenate([_block_diag(rg_wr[layer]), _block_diag(rg_wi[layer])], axis=1).astype(BF16)
        proj_consts = (
            row(norm_mix[layer]),
            wl[:, o_hg:o_gd].astype(BF16), wl[:, o_gd:o_ba].astype(BF16), wl[:, o_rg:].astype(BF16), wba,
            hg_lb.astype(F32), gd_conv_w[layer].astype(F32),
            pad_lanes(gd_a_log[layer], GD_HEADS), pad_lanes(gd_dt_bias[layer], GD_HEADS),
            rg_conv_w[layer].astype(F32), row(rg_conv_b[layer]), wgate,
            jnp.concatenate([row(rg_br[layer]), row(rg_bi[layer])], axis=1), row(rg_lambda[layer]),
        )
        hg_pre, gd_pre, ba, rg_pre = _proj_call(xf, proj_consts, layer=layer,
                                                tiles_per_batch=seq // tm_proj, tm=tm_proj)
        hg_nw = jnp.tile(hg_norm_w[layer].astype(F32), HG_HEADS).reshape(1, HG_W)
        o_hgv = _hg_call(hg_pre, hg_consts_fixed + (hg_nw,), batch=batch, tl=tl)
        o_gdv = _gd_call(gd_pre, ba, (ltri, row(gd_norm_w[layer])), batch=batch, tl=tl)
        o_rgv = _rg_call(rg_pre, batch=batch, tl=tl)

        mix = (o_hgv, o_gdv, o_rgv, w_out[layer].astype(BF16))
        fnw = row(norm_final) if layer == depth - 1 else None
        xf = _ffn_call(xf, mix, row(norm_ffn2[layer]), ffn2_gate[layer].astype(BF16),
                       ffn2_up[layer].astype(BF16), ffn2_down[layer].astype(BF16), fnw, tm=tm_ffn)
    return xf.reshape(batch, seq, d)
```

```python
import functools

import numpy as np
import jax
import jax.numpy as jnp
from jax import lax
from jax.experimental import pallas as pl
from jax.experimental.pallas import tpu as pltpu

F32 = jnp.float32
BF16 = jnp.bfloat16

EPS = 1e-6
CHUNK = 64
CONV_W = 4
HG_HEADS, HG_DK, HG_DV = 4, 64, 64
GD_HEADS, GD_DK, GD_DV = 4, 128, 128
RG_BLOCKS = 4
RG_C = 8.0
HG_W = HG_HEADS * HG_DV
HG_QK = HG_HEADS * HG_DK
GD_W = GD_HEADS * GD_DV
GD_QK = GD_HEADS * GD_DK
LANES = 128
HALO = 8
VMEM_LIMIT = 56 * 1024 * 1024

HG_LEVELS = (32, 16, 8, 4, 2, 1)
HG_MATMUL_LEVELS = (4, 2)


def _dot(a, b):
    return jnp.dot(a, b, preferred_element_type=F32)


def _dot_nt(a, b):
    return lax.dot_general(a, b, (((1,), (1,)), ((), ())), preferred_element_type=F32)


def _dot_tn(a, b):
    return lax.dot_general(a, b, (((0,), (0,)), ((), ())), preferred_element_type=F32)


def _split3(x):
    hi = x.astype(BF16)
    r = x - hi.astype(F32)
    mid = r.astype(BF16)
    lo = (r - mid.astype(F32)).astype(BF16)
    return hi, mid, lo


def _dot_exact_left(m, x):
    hi, mid, lo = _split3(x)
    return _dot(m, hi) + _dot(m, mid) + _dot(m, lo)


def _dot_exact_right(x, m):
    hi = x.astype(BF16)
    lo = (x - hi.astype(F32)).astype(BF16)
    return _dot(hi, m) + _dot(lo, m)


def _dot_slabs(a, w_ref):
    return jnp.concatenate([_dot(a, w_ref[j]) for j in range(w_ref.shape[0])], axis=1)


def _col_slabs(w, width=512):
    k, n = w.shape
    return w.reshape(k, n // width, width).transpose(1, 0, 2)


def _rms(x, w):
    ms = jnp.mean(x * x, axis=-1, keepdims=True)
    return x * lax.rsqrt(ms + EPS) * w


def _sigmoid(x):
    return 1.0 / (1.0 + jnp.exp(-x))


def _silu(x):
    return x * _sigmoid(x)


def _softplus(x):
    return jnp.maximum(x, 0.0) + jnp.log1p(jnp.exp(-jnp.abs(x)))


def _ffn_kernel(*refs, has_mix, has_final, fc):
    it = iter(refs)
    x_ref = next(it)
    if has_mix:
        ohg_ref, ogd_ref, org_ref, wout_ref = next(it), next(it), next(it), next(it)
    nw_ref, wg_ref, wu_ref, wd_ref = next(it), next(it), next(it), next(it)
    fnw_ref = next(it) if has_final else None
    o_ref = next(it)
    a_ref = next(it)

    x = x_ref[...]
    if has_mix:
        ohg, ogd, org = (r[...].astype(BF16) for r in (ohg_ref, ogd_ref, org_ref))
        x = x + jnp.concatenate(
            [_dot(ohg, wout_ref[j, 0:HG_W, :]) + _dot(ogd, wout_ref[j, HG_W:HG_W + GD_W, :])
             + _dot(org, wout_ref[j, HG_W + GD_W:, :]) for j in range(wout_ref.shape[0])], axis=1)
    h = _rms(x, nw_ref[...]).astype(BF16)
    d_ff = wg_ref.shape[1]
    for c in range(d_ff // fc):
        sl = slice(c * fc, (c + 1) * fc)
        g = _dot(h, wg_ref[:, sl])
        u = _dot(h, wu_ref[:, sl])
        a_ref[:, sl] = (_silu(g) * u).astype(BF16)
    y = x + 0.5 * _dot_slabs(a_ref[...], wd_ref)
    if has_final:
        y = _rms(y, fnw_ref[...])
    o_ref[...] = y


def _const_spec(shape):
    nd = len(shape)
    return pl.BlockSpec(shape, lambda *_: (0,) * nd, pipeline_mode=pl.Buffered(1))


def _ffn_call(x, mix, nw, wg, wu, wd, fnw, *, tm):
    t, d = x.shape
    d_ff = wg.shape[1]
    has_mix, has_final = mix is not None, fnw is not None
    row = lambda w: pl.BlockSpec((tm, w), lambda i: (i, 0))
    args, specs = [x], [row(d)]
    if has_mix:
        ohg, ogd, org, wout = mix
        args += [ohg, ogd, org, wout]
        specs += [row(HG_W), row(GD_W), row(org.shape[1]), _const_spec(wout.shape)]
    args += [nw, wg, wu, wd]
    specs += [_const_spec(nw.shape), _const_spec(wg.shape), _const_spec(wu.shape), _const_spec(wd.shape)]
    if has_final:
        args.append(fnw)
        specs.append(_const_spec(fnw.shape))
    kern = functools.partial(_ffn_kernel, has_mix=has_mix, has_final=has_final, fc=256)
    return pl.pallas_call(
        kern,
        grid=(t // tm,),
        in_specs=specs,
        out_specs=row(d),
        out_shape=jax.ShapeDtypeStruct((t, d), F32),
        scratch_shapes=[pltpu.VMEM((tm, d_ff), BF16)],
        compiler_params=pltpu.CompilerParams(dimension_semantics=("arbitrary",),
                                             vmem_limit_bytes=VMEM_LIMIT),
        name="ffn_mix" if has_mix else "ffn",
    )(*args)


def _causal_conv(cur, halo, w_ref):
    tm, c = cur.shape
    row8 = lax.broadcasted_iota(jnp.int32, (1, HALO, 1), 1)
    x3 = jnp.concatenate([halo, cur], axis=0).reshape(tm // HALO + 1, HALO, c)
    y = w_ref[CONV_W - 1:CONV_W, :] * cur
    for s in range(1, CONV_W):
        r = pltpu.roll(x3, s, 1)
        shifted = jnp.where(row8 < s, r[:-1], r[1:]).reshape(tm, c)
        y = y + w_ref[CONV_W - 1 - s:CONV_W - s, :] * shifted
    return y


def _proj_kernel(x_ref, nw_ref, whg_ref, wgd_ref, wrg_ref, wba_ref, gdcw_ref,
                 alog_ref, dtb_ref, rgcw_ref, rgcb_ref, wgate_ref, bgate_ref, lam_ref,
                 hg_ref, gd_ref, ba_ref, rg_ref, gdhalo_ref, rghalo_ref,
                 *, tiles_per_batch, tm):
    i = pl.program_id(0)
    h = _rms(x_ref[...], nw_ref[...]).astype(BF16)

    hg_ref[...] = _dot_slabs(h, whg_ref)

    @pl.when((i % tiles_per_batch) == 0)
    def _():
        gdhalo_ref[...] = jnp.zeros(gdhalo_ref.shape, F32)
        rghalo_ref[...] = jnp.zeros(rghalo_ref.shape, F32)

    g = _dot_slabs(h, wgd_ref)
    ncv = 2 * GD_QK + GD_W
    gcur = g[:, 0:ncv]
    c = _silu(_causal_conv(gcur, gdhalo_ref[...], gdcw_ref))
    gdhalo_ref[...] = gcur[tm - HALO:tm, :]
    for hh in range(GD_HEADS):
        sq = c[:, hh * GD_DK:(hh + 1) * GD_DK]
        sk = c[:, GD_QK + hh * GD_DK:GD_QK + (hh + 1) * GD_DK]
        gd_ref[:, hh * GD_DK:(hh + 1) * GD_DK] = (
            sq * lax.rsqrt(jnp.sum(sq * sq, axis=-1, keepdims=True) + EPS) * (GD_DK ** -0.5))
        gd_ref[:, GD_QK + hh * GD_DK:GD_QK + (hh + 1) * GD_DK] = (
            sk * lax.rsqrt(jnp.sum(sk * sk, axis=-1, keepdims=True) + EPS))
    gd_ref[:, 2 * GD_QK:ncv] = c[:, 2 * GD_QK:ncv]
    gd_ref[:, ncv:] = _silu(g[:, ncv:])

    ba = _dot(h, wba_ref[...])
    lane = lax.broadcasted_iota(jnp.int32, (1, LANES), 1)
    gg = -jnp.exp(alog_ref[...]) * _softplus(ba + dtb_ref[...])
    ba_ref[...] = jnp.where(lane < GD_HEADS, _sigmoid(ba), gg)

    r = _dot(h, wrg_ref[...])
    rgw = rghalo_ref.shape[1]
    rcur = r[:, 0:rgw]
    xc = _causal_conv(rcur, rghalo_ref[...], rgcw_ref) + rgcb_ref[...]
    rghalo_ref[...] = rcur[tm - HALO:tm, :]
    gates = _dot(xc.astype(BF16), wgate_ref[...]) + bgate_ref[...]
    rr = _sigmoid(gates[:, 0:rgw])
    ig = _sigmoid(gates[:, rgw:])
    log_a = -RG_C * rr * _softplus(-lam_ref[...])
    a = jnp.exp(log_a)
    rg_ref[:, 0:rgw] = a
    rg_ref[:, rgw:2 * rgw] = jnp.sqrt(-jnp.tanh(log_a) * (a * a + 1.0)) * (ig * xc)
    rg_ref[:, 2 * rgw:] = jax.nn.gelu(r[:, rgw:], approximate=True)


def _proj_call(x, consts, *, tiles_per_batch, tm):
    t, d = x.shape
    rgw = consts[8].shape[1]
    row = lambda w: pl.BlockSpec((tm, w), lambda i: (i, 0))
    kern = functools.partial(_proj_kernel, tiles_per_batch=tiles_per_batch, tm=tm)
    ncv = 2 * GD_QK + GD_W
    return pl.pallas_call(
        kern,
        grid=(t // tm,),
        in_specs=[row(d)] + [_const_spec(c.shape) for c in consts],
        out_specs=[row(2 * HG_QK + 2 * HG_W), row(ncv + GD_W), row(LANES), row(3 * rgw)],
        out_shape=[jax.ShapeDtypeStruct((t, 2 * HG_QK + 2 * HG_W), F32),
                   jax.ShapeDtypeStruct((t, ncv + GD_W), F32),
                   jax.ShapeDtypeStruct((t, LANES), F32),
                   jax.ShapeDtypeStruct((t, 3 * rgw), F32)],
        scratch_shapes=[pltpu.VMEM((HALO, ncv), F32), pltpu.VMEM((HALO, rgw), F32)],
        compiler_params=pltpu.CompilerParams(dimension_semantics=("arbitrary",),
                                             vmem_limit_bytes=VMEM_LIMIT),
        name="mixer_proj",
    )(x, *consts)


def _hg_constants():
    c = CHUNK
    mats, masks = {}, []
    for hsz in HG_LEVELS:
        m = np.zeros((c, c), np.float32)
        pm = np.zeros((c, c), np.float32)
        for t in range(c):
            blk = (t // (2 * hsz)) * (2 * hsz)
            b = blk + hsz - 1
            if t - blk >= hsz:
                m[t, b + 1:t + 1] = 1.0
                pm[t, blk:blk + hsz] = 1.0
            else:
                m[t, t + 1:b + 1] = 1.0
        mats[hsz] = m
        masks.append(pm)
    masks.append(np.eye(c, dtype=np.float32))
    mfine = np.concatenate([mats[h] for h in HG_MATMUL_LEVELS], axis=0)
    pmask = np.stack([np.tile(m, (1, HG_HEADS)) for m in masks], axis=0)
    hid = np.arange(HG_W) // HG_DV
    bd = (hid[:, None] == hid[None, :]).astype(np.float32)
    return mfine, pmask, bd


def _stack_heads_masked(xb, head_masks):
    return jnp.concatenate([jnp.where(m, xb, jnp.zeros_like(xb)) for m in head_masks], axis=0)


def _hg_kernel(pre_ref, lbraw_ref, ltri_ref, mfine_ref, pmask_ref, bd_ref, avg_ref, nw_ref,
               o_ref, st_ref, *, tl, group, layer):
    @pl.when(pl.program_id(1) == 0)
    def _():
        st_ref[...] = jnp.zeros(st_ref.shape, F32)

    nlev = len(HG_LEVELS)
    rowid = lax.broadcasted_iota(jnp.int32, (CHUNK, 1), 0)
    lane_head = lax.broadcasted_iota(jnp.int32, (1, HG_W), 1) // HG_DV
    head_masks = [lane_head == hh for hh in range(HG_HEADS)]
    ltri = ltri_ref[...]
    mfine = mfine_ref[...]
    bd = bd_ref[...]
    avg = avg_ref[...]
    nw = nw_ref[...]

    depth = lbraw_ref.shape[0]
    rows = [lbraw_ref[r:r + 1, :] for r in range(depth)]
    mx = functools.reduce(jnp.maximum, rows)
    es = [jnp.exp(r - mx) for r in rows]
    tot = functools.reduce(lambda a, b: a + b, es)
    cums, run = [], None
    for e in es:
        run = e / tot if run is None else run + e / tot
        cums.append(run)
    lb = cums[layer] - cums[0]

    def level_exponent(hsz, cum, lf, dfine):
        second = (rowid & hsz) != 0
        if hsz in HG_MATMUL_LEVELS:
            i = HG_MATMUL_LEVELS.index(hsz)
            return dfine[i * CHUNK:(i + 1) * CHUNK, :]
        if hsz == 1:
            return jnp.where(second, lf, 0.0)
        cb = jnp.concatenate(
            [jnp.broadcast_to(cum[b + hsz - 1:b + hsz, :], (2 * hsz, HG_QK))
             for b in range(0, CHUNK, 2 * hsz)], axis=0)
        return jnp.where(second, cum - cb, cb - cum)

    def body(gi, carry):
        units = []
        for c in range(group):
            r0 = pl.multiple_of((gi * group + c) * CHUNK, CHUNK)
            p = pre_ref[pl.ds(r0, CHUNK), :]
            u = dict(r0=r0)
            u["q"] = _silu(p[:, 0:HG_QK]) * (HG_DK ** -0.5)
            f = lb + (1.0 - lb) * _sigmoid(p[:, HG_QK:2 * HG_QK])
            u["lf"] = jnp.log(f)
            u["k"] = 1.0 - f
            u["vb"] = p[:, 2 * HG_QK:2 * HG_QK + HG_W].astype(BF16)
            u["gate"] = _silu(p[:, 2 * HG_QK + HG_W:])
            units.append(u)
        for u in units:
            hi, mid, lo = _split3(u["lf"])
            u["cum"] = _dot(ltri, hi) + _dot(ltri, mid) + _dot(ltri, lo)
            u["dfine"] = _dot(mfine, hi) + _dot(mfine, mid)
        for u in units:
            u["ecum"] = jnp.exp(u["cum"])
            last = u["cum"][CHUNK - 1:CHUNK, :]
            u["kt"] = (u["k"] * jnp.exp(last - u["cum"])).astype(BF16)
            u["qe"] = (u["q"] * u["ecum"]).astype(BF16)
            u["vbd"] = _stack_heads_masked(u["vb"], head_masks)
            u["sc"] = _dot_nt(u["q"].astype(BF16),
                              _stack_heads_masked(u["k"].astype(BF16), head_masks)) * pmask_ref[nlev]
        for li, hsz in enumerate(HG_LEVELS):
            second = (rowid & hsz) != 0
            for u in units:
                e = jnp.exp(level_exponent(hsz, u["cum"], u["lf"], u["dfine"]))
                u["xb"] = (jnp.where(second, u["q"], u["k"]) * e).astype(BF16)
            for u in units:
                u["sc"] = u["sc"] + _dot_nt(u["xb"], _stack_heads_masked(u["xb"], head_masks)) * pmask_ref[li]
        for u in units:
            u["oi"] = _dot(u["sc"].astype(BF16), u["vbd"])
            u["upd"] = _dot_tn(u["vb"], u["kt"]) * bd
        for u in units:
            st = st_ref[...]
            u["o"] = u["oi"] + _dot_nt(u["qe"], st.astype(BF16))
            st_ref[...] = st * u["ecum"][CHUNK - 1:CHUNK, :] + u["upd"]
        for u in units:
            o = u["o"]
            ms = _dot_exact_right(o * o, avg)
            o_ref[pl.ds(u["r0"], CHUNK), :] = o * lax.rsqrt(ms + EPS) * nw * u["gate"]
        return carry

    lax.fori_loop(0, tl // (CHUNK * group), body, 0)


def _hg_call(pre, consts, *, batch, tl, layer):
    t = pre.shape[0]
    nt = t // batch // tl
    kern = functools.partial(_hg_kernel, tl=tl, group=_pick_tile(tl // CHUNK, 4), layer=layer)
    return pl.pallas_call(
        kern,
        grid=(batch, nt),
        in_specs=[pl.BlockSpec((tl, pre.shape[1]), lambda b, i: (b * nt + i, 0))]
        + [_const_spec(c.shape) for c in consts],
        out_specs=pl.BlockSpec((tl, HG_W), lambda b, i: (b * nt + i, 0)),
        out_shape=jax.ShapeDtypeStruct((t, HG_W), F32),
        scratch_shapes=[pltpu.VMEM((HG_W, HG_QK), F32)],
        compiler_params=pltpu.CompilerParams(dimension_semantics=("arbitrary", "arbitrary"),
                                             vmem_limit_bytes=VMEM_LIMIT),
        name="hgrn2",
    )(pre, *consts)


def _gd_kernel(pre_ref, ba_ref, ltri_ref, nw_ref, o_ref, s_ref, *, tl, group):
    @pl.when(pl.program_id(1) == 0)
    def _():
        s_ref[...] = jnp.zeros(s_ref.shape, F32)

    ti = lax.broadcasted_iota(jnp.int32, (CHUNK, CHUNK), 0)
    si = lax.broadcasted_iota(jnp.int32, (CHUNK, CHUNK), 1)
    incl = ti >= si
    strict = ti > si
    ltri = ltri_ref[...]
    nw = nw_ref[...]
    heads = range(GD_HEADS)

    def body(gi, carry):
        units = []
        for c in range(group):
            r0 = pl.multiple_of((gi * group + c) * CHUNK, CHUNK)
            pre = pre_ref[pl.ds(r0, CHUNK), :]
            ba = ba_ref[pl.ds(r0, CHUNK), :]
            cumc = _dot_exact_left(ltri, ba)
            cumr = cumc.T
            for hh in heads:
                u = dict(r0=r0, hh=hh)
                u["q"] = pre[:, hh * GD_DK:(hh + 1) * GD_DK]
                u["k"] = pre[:, GD_QK + hh * GD_DK:GD_QK + (hh + 1) * GD_DK]
                u["v"] = pre[:, 2 * GD_QK + hh * GD_DV:2 * GD_QK + (hh + 1) * GD_DV]
                u["zg"] = pre[:, 2 * GD_QK + GD_W + hh * GD_DV:2 * GD_QK + GD_W + (hh + 1) * GD_DV]
                u["beta"] = ba[:, hh:hh + 1]
                u["cc"] = cumc[:, GD_HEADS + hh:GD_HEADS + hh + 1]
                cr = cumr[GD_HEADS + hh:GD_HEADS + hh + 1, :]
                u["decay"] = jnp.exp(jnp.where(incl, u["cc"] - cr, -1e30))
                u["kb"] = u["k"] * u["beta"]
                u["kbf"] = u["k"].astype(BF16)
                units.append(u)
        for u in units:
            u["a"] = jnp.where(strict, _dot_nt(u["kb"].astype(BF16), u["kbf"]) * u["decay"], 0.0)
            u["qk"] = (_dot_nt(u["q"].astype(BF16), u["kbf"]) * u["decay"]).astype(BF16)
            u["n"] = -u["a"]
        for _ in range(5):
            for u in units:
                apb = u["a"].astype(BF16)
                u["a"] = _dot(apb, apb)
            for u in units:
                u["n"] = u["n"] + u["a"] + _dot(u["n"].astype(BF16), u["a"].astype(BF16))
        for u in units:
            ecc = jnp.exp(u["cc"])
            rhs = jnp.concatenate([u["v"] * u["beta"], u["kb"] * ecc], axis=1)
            sol = rhs + _dot(u["n"].astype(BF16), rhs.astype(BF16))
            u["u"] = sol[:, 0:GD_DV]
            u["w"] = sol[:, GD_DV:].astype(BF16)
            u["qe"] = (u["q"] * ecc).astype(BF16)
            last = u["cc"][CHUNK - 1:CHUNK, :]
            u["kt"] = (u["k"] * jnp.exp(last - u["cc"])).astype(BF16)
            u["elast"] = jnp.exp(last)

        for c in range(group):
            cu = units[c * GD_HEADS:(c + 1) * GD_HEADS]
            for u in cu:
                u["s"] = s_ref[u["hh"]]
                u["sb"] = u["s"].astype(BF16)
            for u in cu:
                u["ws"] = _dot(u["w"], u["sb"])
                u["qs"] = _dot(u["qe"], u["sb"])
            for u in cu:
                u["vnb"] = (u["u"] - u["ws"]).astype(BF16)
            for u in cu:
                u["o"] = u["qs"] + _dot(u["qk"], u["vnb"])
                s_ref[u["hh"]] = u["s"] * u["elast"] + _dot_tn(u["kt"], u["vnb"])
            for u in cu:
                o = u["o"]
                on = o * lax.rsqrt(jnp.mean(o * o, axis=-1, keepdims=True) + EPS) * nw * u["zg"]
                o_ref[pl.ds(u["r0"], CHUNK), u["hh"] * GD_DV:(u["hh"] + 1) * GD_DV] = on
        return carry

    lax.fori_loop(0, tl // (CHUNK * group), body, 0)


def _gd_call(pre, ba, consts, *, batch, tl):
    t = pre.shape[0]
    nt = t // batch // tl
    kern = functools.partial(_gd_kernel, tl=tl, group=_pick_tile(tl // CHUNK, 4))
    rows = lambda w: pl.BlockSpec((tl, w), lambda b, i: (b * nt + i, 0))
    return pl.pallas_call(
        kern,
        grid=(batch, nt),
        in_specs=[rows(pre.shape[1]), rows(ba.shape[1])] + [_const_spec(c.shape) for c in consts],
        out_specs=rows(GD_W),
        out_shape=jax.ShapeDtypeStruct((t, GD_W), F32),
        scratch_shapes=[pltpu.VMEM((GD_HEADS, GD_DK, GD_DV), F32)],
        compiler_params=pltpu.CompilerParams(dimension_semantics=("arbitrary", "arbitrary"),
                                             vmem_limit_bytes=VMEM_LIMIT),
        name="gated_delta",
    )(pre, ba, *consts)


def _rg_kernel(pre_ref, o_ref, carry_ref, *, tl):
    @pl.when(pl.program_id(1) == 0)
    def _():
        carry_ref[...] = jnp.zeros(carry_ref.shape, F32)

    w = o_ref.shape[1]
    rowid = lax.broadcasted_iota(jnp.int32, (CHUNK, 1), 0)

    def body(ci, carry):
        r0 = pl.multiple_of(ci * CHUNK, CHUNK)
        pre = pre_ref[pl.ds(r0, CHUNK), :]
        a = pre[:, 0:w]
        b = pre[:, w:2 * w]
        gate = pre[:, 2 * w:]
        s = 1
        while s < CHUNK:
            keep = rowid >= s
            a_sh = jnp.where(keep, pltpu.roll(a, s, 0), 1.0)
            b_sh = jnp.where(keep, pltpu.roll(b, s, 0), 0.0)
            b = a * b_sh + b
            a = a * a_sh
            s *= 2
        hst = b + a * carry_ref[0:1, :]
        carry_ref[0:1, :] = hst[CHUNK - 1:CHUNK, :]
        o_ref[pl.ds(r0, CHUNK), :] = hst * gate
        return carry

    lax.fori_loop(0, tl // CHUNK, body, 0)


def _rg_call(pre, *, batch, tl):
    t = pre.shape[0]
    w = pre.shape[1] // 3
    nt = t // batch // tl
    kern = functools.partial(_rg_kernel, tl=tl)
    return pl.pallas_call(
        kern,
        grid=(batch, nt),
        in_specs=[pl.BlockSpec((tl, 3 * w), lambda b, i: (b * nt + i, 0))],
        out_specs=pl.BlockSpec((tl, w), lambda b, i: (b * nt + i, 0)),
        out_shape=jax.ShapeDtypeStruct((t, w), F32),
        scratch_shapes=[pltpu.VMEM((HALO, w), F32)],
        compiler_params=pltpu.CompilerParams(dimension_semantics=("arbitrary", "arbitrary"),
                                             vmem_limit_bytes=VMEM_LIMIT),
        name="rg_lru",
    )(pre)


def _block_diag(w):
    n, i, o = w.shape
    eye = jnp.eye(n, dtype=w.dtype)
    return (eye[:, None, :, None] * w[:, :, None, :]).reshape(n * i, n * o)


def _pick_tile(n, target):
    t = min(n, target)
    while n % t:
        t //= 2
    return t


def kernel(x, norm_ffn1, ffn1_gate, ffn1_up, ffn1_down, norm_mix, w_in, hg_lb, hg_norm_w, gd_conv_w, gd_a_log, gd_dt_bias, gd_norm_w, rg_conv_w, rg_conv_b, rg_wr, rg_br, rg_wi, rg_bi, rg_lambda, w_out, norm_ffn2, ffn2_gate, ffn2_up, ffn2_down, norm_final):
    batch, seq, d = x.shape
    depth = w_in.shape[0]
    t = batch * seq
    rgw = rg_conv_w.shape[-1]
    assert seq % CHUNK == 0 and d == HG_W + GD_W + rgw
    assert w_in.shape[-1] == 2 * HG_QK + 2 * HG_W + 2 * GD_QK + 2 * GD_W + 2 * GD_HEADS + 2 * rgw

    tm_ffn = _pick_tile(seq, 512)
    tm_proj = _pick_tile(seq, 256)
    tl = _pick_tile(seq, 512)

    mfine, pmask, bd = _hg_constants()
    ltri = jnp.asarray(np.tril(np.ones((CHUNK, CHUNK), np.float32)), BF16)
    hg_consts_fixed = (hg_lb.astype(F32), ltri, jnp.asarray(mfine, BF16), jnp.asarray(pmask, F32),
                       jnp.asarray(bd, F32), jnp.asarray(bd / HG_DV, BF16))

    o_hg = 0
    o_gd = o_hg + 2 * HG_QK + 2 * HG_W
    o_ba = o_gd + 2 * GD_QK + 2 * GD_W
    o_rg = o_ba + 2 * GD_HEADS

    xf = x.reshape(t, d)
    mix = None
    for layer in range(depth):
        row = lambda v: v.astype(F32).reshape(1, -1)
        xf = _ffn_call(xf, None, row(norm_ffn1[layer]), ffn1_gate[layer].astype(BF16),
                       ffn1_up[layer].astype(BF16), _col_slabs(ffn1_down[layer].astype(BF16)), None,
                       tm=tm_ffn)

        wl = w_in[layer]
        pad_lanes = lambda v, off: jnp.zeros((1, LANES), F32).at[0, off:off + v.shape[0]].set(v.astype(F32))
        wba = jnp.zeros((d, LANES), F32).at[:, 0:2 * GD_HEADS].set(wl[:, o_ba:o_rg]).astype(BF16)
        wgate = jnp.concatenate([_block_diag(rg_wr[layer]), _block_diag(rg_wi[layer])], axis=1).astype(BF16)
        proj_consts = (
            row(norm_mix[layer]),
            _col_slabs(wl[:, o_hg:o_gd].astype(BF16)), _col_slabs(wl[:, o_gd:o_ba].astype(BF16)),
            wl[:, o_rg:].astype(BF16), wba,
            gd_conv_w[layer].astype(F32),
            pad_lanes(gd_a_log[layer], GD_HEADS), pad_lanes(gd_dt_bias[layer], GD_HEADS),
            rg_conv_w[layer].astype(F32), row(rg_conv_b[layer]), wgate,
            jnp.concatenate([row(rg_br[layer]), row(rg_bi[layer])], axis=1), row(rg_lambda[layer]),
        )
        hg_pre, gd_pre, ba, rg_pre = _proj_call(xf, proj_consts,
                                                tiles_per_batch=seq // tm_proj, tm=tm_proj)
        hg_nw = jnp.tile(hg_norm_w[layer].astype(F32), HG_HEADS).reshape(1, HG_W)
        o_hgv = _hg_call(hg_pre, hg_consts_fixed + (hg_nw,), batch=batch, tl=tl, layer=layer)
        o_gdv = _gd_call(gd_pre, ba, (ltri, row(gd_norm_w[layer])), batch=batch, tl=tl)
        o_rgv = _rg_call(rg_pre, batch=batch, tl=tl)

        mix = (o_hgv, o_gdv, o_rgv, _col_slabs(w_out[layer].astype(BF16)))
        fnw = row(norm_final) if layer == depth - 1 else None
        xf = _ffn_call(xf, mix, row(norm_ffn2[layer]), ffn2_gate[layer].astype(BF16),
                       ffn2_up[layer].astype(BF16), _col_slabs(ffn2_down[layer].astype(BF16)), fnw,
                       tm=tm_ffn)
    return xf.reshape(batch, seq, d)
```
